```python
import jax, jax.numpy as jnp
from jax import lax
import numpy as np

D_MODEL = 2048
BATCH = 1
SEQ = 16384
DEPTH = 2

ATTN_WIDTH = D_MODEL // 2
RET_WIDTH = D_MODEL - ATTN_WIDTH
MIX_WIDTH = ATTN_WIDTH + RET_WIDTH
ATTN_HEAD_DIM = 64
N_ATTN_HEADS = ATTN_WIDTH // ATTN_HEAD_DIM
N_KV_HEADS = N_ATTN_HEADS // 8
WINDOW = 128
ATTN_BLOCK = 128
RET_QK_DIM = 256
RET_V_DIM = 256
N_RET_HEADS = RET_WIDTH // RET_V_DIM
RET_CHUNK = 128
D_FF = 5632
N_SUBLAYERS = 3
N_MOD = 3
NORM_EPS = 1e-6
GN_EPS = 1e-5

IN_SIZES = [
    N_ATTN_HEADS * ATTN_HEAD_DIM,
    N_KV_HEADS * ATTN_HEAD_DIM,
    N_KV_HEADS * ATTN_HEAD_DIM,
    N_RET_HEADS * RET_QK_DIM,
    N_RET_HEADS * RET_QK_DIM,
    N_RET_HEADS * RET_V_DIM,
    N_RET_HEADS * RET_V_DIM,
]
IN_WIDTH = int(sum(IN_SIZES))
IN_OFFSETS = [int(o) for o in np.cumsum(IN_SIZES)[:-1]]

kernel_name = "hymba_swa_sink_retention_macaron"


def rms_norm(x, gain):
    x32 = x.astype(jnp.float32)
    y = x32 * lax.rsqrt(jnp.mean(x32 * x32, axis=-1, keepdims=True) + NORM_EPS)
    return (y * gain.astype(jnp.float32)).astype(x.dtype)


def modulate(h, shift, scale):
    return h * (1.0 + scale[:, None, :]) + shift[:, None, :]


def swiglu(h, w_gate, w_up, w_down):
    return (jax.nn.silu(h @ w_gate) * (h @ w_up)) @ w_down


def alibi_slopes(n_heads):
    return jnp.asarray(2.0 ** (-8.0 * (np.arange(n_heads) + 1) / n_heads), dtype=jnp.float32)


def sliding_window_attention(q, k, v, sinks):
    B, S, H, dh = q.shape
    W = ATTN_BLOCK
    nb = S // W
    G = H // N_KV_HEADS
    f32 = jnp.float32
    qb = q.astype(f32).reshape(B, nb, W, N_KV_HEADS, G, dh)
    kb = k.astype(f32).reshape(B, nb, W, N_KV_HEADS, dh)
    vb = v.astype(f32).reshape(B, nb, W, N_KV_HEADS, dh)
    prev = lambda t: jnp.concatenate([jnp.zeros_like(t[:, :1]), t[:, :-1]], axis=1)
    keys = jnp.concatenate([prev(kb), kb], axis=2)
    vals = jnp.concatenate([prev(vb), vb], axis=2)
    scores = jnp.einsum('bnqkgd,bnskd->bnkgqs', qb, keys) * (dh ** -0.5)
    qi = jnp.arange(W)[:, None]
    kj = jnp.arange(2 * W)[None, :]
    dist = qi + W - kj
    key_pos = jnp.arange(nb)[:, None, None] * W + kj - W
    valid = (dist >= 0) & (dist < WINDOW) & (key_pos >= 0)
    slopes = alibi_slopes(H).reshape(N_KV_HEADS, G)[:, :, None, None]
    scores = scores - slopes * dist.astype(f32)
    scores = jnp.where(valid[None, :, None, None], scores, -jnp.inf)
    sink = sinks.astype(f32).reshape(N_KV_HEADS, G)[:, :, None, None]
    m = jnp.maximum(jnp.max(scores, axis=-1, keepdims=True), sink)
    p = jnp.exp(scores - m)
    denom = jnp.sum(p, axis=-1, keepdims=True) + jnp.exp(sink - m)
    out = jnp.einsum('bnkgqs,bnskd->bnqkgd', p / denom, vals)
    return out.reshape(B, S, H * dh)


def retention(q, k, v):
    B, S, H, dk = q.shape
    dv = v.shape[-1]
    C = RET_CHUNK
    nc = S // C
    f32 = jnp.float32
    log_gamma = jnp.log(1.0 - 2.0 ** (-5.0 - jnp.arange(H, dtype=f32)))
    qc = q.astype(f32).reshape(B, nc, C, H, dk)
    kc = k.astype(f32).reshape(B, nc, C, H, dk) * (dk ** -0.5)
    vc = v.astype(f32).reshape(B, nc, C, H, dv)
    idx = jnp.arange(C, dtype=f32)
    rel = idx[:, None] - idx[None, :]
    decay_intra = jnp.where(rel >= 0, jnp.exp(log_gamma[:, None, None] * jnp.maximum(rel, 0.0)), 0.0)
    scores = jnp.einsum('bnihd,bnjhd->bnhij', qc, kc) * decay_intra
    inner = jnp.einsum('bnhij,bnjhe->bnihe', scores, vc)
    zeta = jnp.exp(log_gamma[:, None] * (C - 1.0 - idx)[None, :])
    kz = kc * zeta.T[None, None, :, :, None]
    kv = jnp.einsum('bnjhd,bnjhe->nbhde', kz, vc)
    chunk_decay = jnp.exp(log_gamma * C)[None, :, None, None]

    def step(state, kv_n):
        return chunk_decay * state + kv_n, state

    _, states = lax.scan(step, jnp.zeros((B, H, dk, dv), f32), kv)
    xi = jnp.exp(log_gamma[:, None] * (idx + 1.0)[None, :])
    cross = jnp.einsum('bnihd,nbhde->bnihe', qc, states) * xi.T[None, None, :, :, None]
    out = inner + cross
    mu = jnp.mean(out, axis=-1, keepdims=True)
    var = jnp.mean(jnp.square(out - mu), axis=-1, keepdims=True)
    out = (out - mu) * lax.rsqrt(var + GN_EPS)
    return out.reshape(B, S, H, dv)


def token_mixing(h, w_in, w_out, sinks):
    B, S, _ = h.shape
    proj = h @ w_in
    q_a, k_a, v_a, q_r, k_r, v_r, g_r = jnp.split(proj, IN_OFFSETS, axis=-1)
    y_a = sliding_window_attention(
        q_a.reshape(B, S, N_ATTN_HEADS, ATTN_HEAD_DIM),
        k_a.reshape(B, S, N_KV_HEADS, ATTN_HEAD_DIM),
        v_a.reshape(B, S, N_KV_HEADS, ATTN_HEAD_DIM),
        sinks).astype(h.dtype)
    y_r = retention(
        q_r.reshape(B, S, N_RET_HEADS, RET_QK_DIM),
        k_r.reshape(B, S, N_RET_HEADS, RET_QK_DIM),
        v_r.reshape(B, S, N_RET_HEADS, RET_V_DIM)).reshape(B, S, RET_WIDTH)
    y_r = (jax.nn.silu(g_r.astype(jnp.float32)) * y_r).astype(h.dtype)
    return jnp.concatenate([y_a, y_r], axis=-1) @ w_out


def sublayer(x, fn, g_pre, g_post, shift, scale, gate, res_weight):
    h = modulate(rms_norm(x, g_pre), shift, scale)
    return x + res_weight * gate[:, None, :] * rms_norm(fn(h), g_post)


def setup_inputs(seed: int = 0) -> dict:
    key = jax.random.key(seed)
    ks = jax.random.split(key, 16)
    f32 = jnp.float32
    n = lambda k, shape, s: jax.random.normal(k, shape, f32) * s
    D, F = D_MODEL, D_FF
    return {
        "x": n(ks[0], (BATCH, SEQ, D), 1.0),
        "c": n(ks[1], (BATCH, D), 1.0),
        "w_ada": n(ks[2], (DEPTH, D, N_SUBLAYERS * N_MOD * D), 0.5 * D ** -0.5),
        "b_ada": n(ks[3], (DEPTH, N_SUBLAYERS * N_MOD * D), 0.01),
        "norm_pre": 1.0 + n(ks[4], (DEPTH, N_SUBLAYERS, D), 0.05),
        "norm_post": 1.0 + n(ks[5], (DEPTH, N_SUBLAYERS, D), 0.05),
        "ffn1_gate": n(ks[6], (DEPTH, D, F), D ** -0.5),
        "ffn1_up": n(ks[7], (DEPTH, D, F), D ** -0.5),
        "ffn1_down": n(ks[8], (DEPTH, F, D), F ** -0.5),
        "w_in": n(ks[9], (DEPTH, D, IN_WIDTH), D ** -0.5),
        "w_out": n(ks[10], (DEPTH, MIX_WIDTH, D), MIX_WIDTH ** -0.5),
        "attn_sinks": n(ks[11], (DEPTH, N_ATTN_HEADS), 0.5),
        "ffn2_gate": n(ks[12], (DEPTH, D, F), D ** -0.5),
        "ffn2_up": n(ks[13], (DEPTH, D, F), D ** -0.5),
        "ffn2_down": n(ks[14], (DEPTH, F, D), F ** -0.5),
    }


def reference(x, c, w_ada, b_ada, norm_pre, norm_post, ffn1_gate, ffn1_up, ffn1_down,
              w_in, w_out, attn_sinks, ffn2_gate, ffn2_up, ffn2_down):
    B = x.shape[0]
    c_act = jax.nn.silu(c)
    for l in range(DEPTH):
        mods = (c_act @ w_ada[l] + b_ada[l]).reshape(B, N_SUBLAYERS, N_MOD, D_MODEL)
        x = sublayer(x, lambda h: swiglu(h, ffn1_gate[l], ffn1_up[l], ffn1_down[l]),
                     norm_pre[l, 0], norm_post[l, 0],
                     mods[:, 0, 0], mods[:, 0, 1], mods[:, 0, 2], 0.5)
        x = sublayer(x, lambda h: token_mixing(h, w_in[l], w_out[l], attn_sinks[l]),
                     norm_pre[l, 1], norm_post[l, 1],
                     mods[:, 1, 0], mods[:, 1, 1], mods[:, 1, 2], 1.0)
        x = sublayer(x, lambda h: swiglu(h, ffn2_gate[l], ffn2_up[l], ffn2_down[l]),
                     norm_pre[l, 2], norm_post[l, 2],
                     mods[:, 2, 0], mods[:, 2, 1], mods[:, 2, 2], 0.5)
    return x
```

```python
import functools

import jax
import jax.numpy as jnp
import numpy as np
from jax import lax
from jax.experimental import pallas as pl
from jax.experimental.pallas import tpu as pltpu

f32 = jnp.float32
bf16 = jnp.bfloat16

LANES = 128
V7X_VMEM_BYTES = 64 * 1024 * 1024
VMEM_TEMP_BYTES = 10 * 1024 * 1024

D_MODEL = 2048
SEQ = 16384
DEPTH = 2
D_FF = 5632
N_SUBLAYERS = 3
N_MOD = 3
NORM_EPS = 1e-6
GN_EPS = 1e-5

HEAD_DIM = 64
N_ATTN_HEADS = 16
N_KV_HEADS = 2
GROUP = N_ATTN_HEADS // N_KV_HEADS
PAIRS = GROUP // 2
BLOCK = 128
N_RET_HEADS = 4
RET_DIM = 256

ATTN_WIDTH = N_ATTN_HEADS * HEAD_DIM
KV_WIDTH = N_KV_HEADS * HEAD_DIM
RET_WIDTH = N_RET_HEADS * RET_DIM
OFF_QA = 0
OFF_KA = OFF_QA + ATTN_WIDTH
OFF_VA = OFF_KA + KV_WIDTH
OFF_QR = OFF_VA + KV_WIDTH
OFF_KR = OFF_QR + RET_WIDTH
OFF_VR = OFF_KR + RET_WIDTH
OFF_GR = OFF_VR + RET_WIDTH
IN_WIDTH = OFF_GR + RET_WIDTH
MIX_WIDTH = ATTN_WIDTH + RET_WIDTH

MASK_VALUE = -1e30

FFN_TM, FFN_TF = 512, 512
INPROJ_TM, INPROJ_TN = 1024, 1792
OUTPROJ_TM = 512
MIXER_TS = 512
MODS_TN = 1024
ROW_CHUNK = 64

V_GPRE, V_SHIFT, V_SCALE, V_GATE, V_GPOST = 0, 1, 2, 3, 4
VEC_ROWS = 8


def _vmem_limit(buffer_bytes):
    return min(buffer_bytes + VMEM_TEMP_BYTES, V7X_VMEM_BYTES - 2 * 1024 * 1024)


def _silu(v):
    return v / (1.0 + jnp.exp(-v))


def _mods_kernel(c_ref, w_ref, b_ref, o_ref):
    ca = _silu(c_ref[...])
    o_ref[0] = jnp.sum(w_ref[0] * ca, axis=0, keepdims=True) + b_ref[0]


def _mods_call(c_col, w_ada, b_ada3):
    depth, d, n = w_ada.shape
    tn = MODS_TN
    return pl.pallas_call(
        _mods_kernel,
        out_shape=jax.ShapeDtypeStruct((depth, 1, n), f32),
        grid=(depth, n // tn),
        in_specs=[
            pl.BlockSpec((d, 1), lambda l, j: (0, 0)),
            pl.BlockSpec((1, d, tn), lambda l, j: (l, 0, j)),
            pl.BlockSpec((1, 1, tn), lambda l, j: (l, 0, j)),
        ],
        out_specs=pl.BlockSpec((1, 1, tn), lambda l, j: (l, 0, j)),
        compiler_params=pltpu.CompilerParams(
            dimension_semantics=("arbitrary", "arbitrary"),
            vmem_limit_bytes=_vmem_limit(2 * d * tn * 4 + d * LANES * 4 * 2),
        ),
        name="adaln_mods",
    )(c_col, w_ada, b_ada3)


def _prenorm_rows(x_ref, vec_ref, h_ref, rows):
    g = vec_ref[V_GPRE:V_GPRE + 1, :]
    sh = vec_ref[V_SHIFT:V_SHIFT + 1, :]
    sc1 = 1.0 + vec_ref[V_SCALE:V_SCALE + 1, :]

    def body(ci, carry):
        r = pl.multiple_of(ci * ROW_CHUNK, ROW_CHUNK)
        x = x_ref[pl.ds(r, ROW_CHUNK), :]
        ms = jnp.mean(x * x, axis=-1, keepdims=True)
        y = (x * lax.rsqrt(ms + NORM_EPS)) * g
        h_ref[pl.ds(r, ROW_CHUNK), :] = (y * sc1 + sh).astype(h_ref.dtype)
        return carry

    lax.fori_loop(0, rows // ROW_CHUNK, body, 0)


def _postnorm_residual_rows(f_ref, x_ref, vec_ref, o_ref, rows, res_weight):
    gp = vec_ref[V_GPOST:V_GPOST + 1, :]
    gw = res_weight * vec_ref[V_GATE:V_GATE + 1, :]

    def body(ci, carry):
        r = pl.multiple_of(ci * ROW_CHUNK, ROW_CHUNK)
        f = f_ref[pl.ds(r, ROW_CHUNK), :]
        ms = jnp.mean(f * f, axis=-1, keepdims=True)
        y = (f * lax.rsqrt(ms + NORM_EPS)) * gp
        o_ref[pl.ds(r, ROW_CHUNK), :] = x_ref[pl.ds(r, ROW_CHUNK), :] + gw * y
        return carry

    lax.fori_loop(0, rows // ROW_CHUNK, body, 0)


def _ffn_kernel(x_ref, vec_ref, wg_ref, wu_ref, wd_ref, o_ref, h_ref, *, res_weight):
    j = pl.program_id(1)
    tm = x_ref.shape[0]

    @pl.when(j == 0)
    def _():
        _prenorm_rows(x_ref, vec_ref, h_ref, tm)
        o_ref[...] = jnp.zeros_like(o_ref)

    h = h_ref[...]
    gate = jnp.dot(h, wg_ref[...], preferred_element_type=f32)
    up = jnp.dot(h, wu_ref[...], preferred_element_type=f32)
    act = (_silu(gate) * up).astype(bf16)
    o_ref[...] += jnp.dot(act, wd_ref[...], preferred_element_type=f32)

    @pl.when(j == pl.num_programs(1) - 1)
    def _():
        _postnorm_residual_rows(o_ref, x_ref, vec_ref, o_ref, tm, res_weight)


def _ffn_call(x, vec, wg, wu, wd, res_weight):
    s, d = x.shape
    f = wg.shape[1]
    tm, tf = FFN_TM, FFN_TF
    buffers = 2 * tm * d * 4 * 2 + tm * d * 2 + 2 * 3 * d * tf * 2 + 2 * VEC_ROWS * d * 4
    return pl.pallas_call(
        functools.partial(_ffn_kernel, res_weight=res_weight),
        out_shape=jax.ShapeDtypeStruct((s, d), f32),
        grid=(s // tm, f // tf),
        in_specs=[
            pl.BlockSpec((tm, d), lambda i, j: (i, 0)),
            pl.BlockSpec((VEC_ROWS, d), lambda i, j: (0, 0)),
            pl.BlockSpec((d, tf), lambda i, j: (0, j)),
            pl.BlockSpec((d, tf), lambda i, j: (0, j)),
            pl.BlockSpec((tf, d), lambda i, j: (j, 0)),
        ],
        out_specs=pl.BlockSpec((tm, d), lambda i, j: (i, 0)),
        scratch_shapes=[pltpu.VMEM((tm, d), bf16)],
        compiler_params=pltpu.CompilerParams(
            dimension_semantics=("arbitrary", "arbitrary"),
            vmem_limit_bytes=_vmem_limit(buffers),
        ),
        name="ffn_sublayer",
    )(x, vec, wg, wu, wd)


def _inproj_kernel(x_ref, vec_ref, w_ref, o_ref, h_ref):
    @pl.when(pl.program_id(1) == 0)
    def _():
        _prenorm_rows(x_ref, vec_ref, h_ref, x_ref.shape[0])

    o_ref[...] = jnp.dot(h_ref[...], w_ref[...], preferred_element_type=f32).astype(o_ref.dtype)


def _inproj_call(x, vec, w_in):
    s, d = x.shape
    n = w_in.shape[1]
    tm, tn = INPROJ_TM, INPROJ_TN
    buffers = 2 * tm * d * 4 + tm * d * 2 + 2 * d * tn * 2 + 2 * tm * tn * 2 + 2 * VEC_ROWS * d * 4
    return pl.pallas_call(
        _inproj_kernel,
        out_shape=jax.ShapeDtypeStruct((s, n), bf16),
        grid=(s // tm, n // tn),
        in_specs=[
            pl.BlockSpec((tm, d), lambda i, j: (i, 0)),
            pl.BlockSpec((VEC_ROWS, d), lambda i, j: (0, 0)),
            pl.BlockSpec((d, tn), lambda i, j: (0, j)),
        ],
        out_specs=pl.BlockSpec((tm, tn), lambda i, j: (i, j)),
        scratch_shapes=[pltpu.VMEM((tm, d), bf16)],
        compiler_params=pltpu.CompilerParams(
            dimension_semantics=("arbitrary", "arbitrary"),
            vmem_limit_bytes=_vmem_limit(buffers),
        ),
        name="mixer_inproj",
    )(x, vec, w_in)


def _outproj_kernel(y_ref, x_ref, vec_ref, w_ref, o_ref):
    o_ref[...] = jnp.dot(y_ref[...], w_ref[...], preferred_element_type=f32)
    _postnorm_residual_rows(o_ref, x_ref, vec_ref, o_ref, x_ref.shape[0], 1.0)


def _outproj_call(y, x, vec, w_out):
    s, d = x.shape
    k = y.shape[1]
    tm = OUTPROJ_TM
    buffers = 2 * tm * k * 2 + 2 * tm * d * 4 * 2 + 2 * k * d * 2 + 2 * VEC_ROWS * d * 4
    return pl.pallas_call(
        _outproj_kernel,
        out_shape=jax.ShapeDtypeStruct((s, d), f32),
        grid=(s // tm,),
        in_specs=[
            pl.BlockSpec((tm, k), lambda i: (i, 0)),
            pl.BlockSpec((tm, d), lambda i: (i, 0)),
            pl.BlockSpec((VEC_ROWS, d), lambda i: (0, 0)),
            pl.BlockSpec((k, d), lambda i: (0, 0)),
        ],
        out_specs=pl.BlockSpec((tm, d), lambda i: (i, 0)),
        compiler_params=pltpu.CompilerParams(
            dimension_semantics=("arbitrary",),
            vmem_limit_bytes=_vmem_limit(buffers),
        ),
        name="mixer_outproj",
    )(y, x, vec, w_out)


def _attention_block(q_ref, kv_ref, bias, sink_e, sink_o, ones_cols, kh, y_ref, r0):
    kv = kv_ref[pl.ds(r0, 2 * BLOCK), :].astype(f32)
    k2 = kv[:, :LANES] * (HEAD_DIM ** -0.5)
    v2 = kv[:, LANES:]
    k2r = pltpu.roll(k2, HEAD_DIM, axis=1)
    v2r = pltpu.roll(v2, HEAD_DIM, axis=1)
    lo = lax.broadcasted_iota(jnp.int32, (2 * BLOCK, LANES), 1) < HEAD_DIM
    k_lo, k_hi = (k2, k2r) if kh == 0 else (k2r, k2)
    v_lo, v_hi = (v2, v2r) if kh == 0 else (v2r, v2)
    kb = jnp.concatenate([jnp.where(lo, k_lo, 0.0), jnp.where(lo, 0.0, k_hi)], axis=0).astype(bf16)
    vb = jnp.concatenate([jnp.where(lo, v_lo, 0.0), jnp.where(lo, 0.0, v_hi)], axis=0).astype(bf16)
    vb = jnp.concatenate([vb, ones_cols], axis=1)

    qbase = OFF_QA + kh * GROUP * HEAD_DIM
    q = jnp.concatenate(
        [q_ref[pl.ds(r0, BLOCK), qbase + p * LANES:qbase + (p + 1) * LANES] for p in range(PAIRS)],
        axis=0)
    s = lax.dot_general(q, kb, (((1,), (1,)), ((), ())), preferred_element_type=f32) + bias
    nk = 2 * BLOCK
    m_e = jnp.maximum(jnp.max(s[:, :nk], axis=1, keepdims=True), sink_e)
    m_o = jnp.maximum(jnp.max(s[:, nk:], axis=1, keepdims=True), sink_o)
    p = jnp.concatenate([jnp.exp(s[:, :nk] - m_e), jnp.exp(s[:, nk:] - m_o)], axis=1).astype(bf16)
    o = jnp.dot(p, vb, preferred_element_type=f32)
    lo_row = lax.broadcasted_iota(jnp.int32, (PAIRS * BLOCK, LANES), 1) < HEAD_DIM
    denom = o[:, LANES:] + jnp.where(lo_row, jnp.exp(sink_e - m_e), jnp.exp(sink_o - m_o))
    out = o[:, :LANES] / denom
    for pr in range(PAIRS):
        c0 = kh * GROUP * HEAD_DIM + pr * LANES
        y_ref[pl.ds(r0, BLOCK), c0:c0 + LANES] = out[pr * BLOCK:(pr + 1) * BLOCK].astype(y_ref.dtype)


def _retention_chunk(p_ref, dec_ref, zeta_ref, xi_ref, state_ref, hh, chunk_decay, y_ref, r0):
    cols = lambda off: slice(off + hh * RET_DIM, off + (hh + 1) * RET_DIM)
    rows = pl.ds(r0, BLOCK)
    q = p_ref[rows, cols(OFF_QR)]
    kf = p_ref[rows, cols(OFF_KR)].astype(f32) * (RET_DIM ** -0.5)
    v = p_ref[rows, cols(OFF_VR)]
    g = p_ref[rows, cols(OFF_GR)].astype(f32)
    sc = lax.dot_general(q, kf.astype(bf16), (((1,), (1,)), ((), ())), preferred_element_type=f32)
    inner = jnp.dot((sc * dec_ref[hh]).astype(bf16), v, preferred_element_type=f32)
    st = state_ref[hh]
    cross = jnp.dot(q, st.astype(bf16), preferred_element_type=f32) * xi_ref[hh]
    kz = (kf * zeta_ref[hh]).astype(bf16)
    kvn = lax.dot_general(kz, v, (((0,), (0,)), ((), ())), preferred_element_type=f32)
    state_ref[hh] = chunk_decay * st + kvn
    out = inner + cross
    mu = jnp.mean(out, axis=-1, keepdims=True)
    dv = out - mu
    var = jnp.mean(dv * dv, axis=-1, keepdims=True)
    y_ref[rows, cols(ATTN_WIDTH)] = (_silu(g) * (dv * lax.rsqrt(var + GN_EPS))).astype(y_ref.dtype)


def _mixer_kernel(p_ref, bias_ref, sink_ref, ones_ref, dec_ref, zeta_ref, xi_ref, y_ref,
                  kv_ref, state_ref, *, chunk_decays):
    i = pl.program_id(0)
    ts = p_ref.shape[0]

    @pl.when(i == 0)
    def _():
        kv_ref[0:BLOCK, :] = jnp.zeros((BLOCK, 2 * LANES), kv_ref.dtype)
        state_ref[...] = jnp.zeros_like(state_ref)

    kv_ref[BLOCK:BLOCK + ts, :] = p_ref[:, OFF_KA:OFF_KA + 2 * KV_WIDTH]
    ones_cols = ones_ref[...]
    first = (i == 0).astype(jnp.int32)
    for b in range(ts // BLOCK):
        r0 = b * BLOCK
        for kh in range(N_KV_HEADS):
            bias = bias_ref[first, kh] if b == 0 else bias_ref[0, kh]
            _attention_block(p_ref, kv_ref, bias, sink_ref[kh, 0], sink_ref[kh, 1], ones_cols, kh,
                             y_ref, r0)
        for hh in range(N_RET_HEADS):
            _retention_chunk(p_ref, dec_ref, zeta_ref, xi_ref, state_ref, hh, chunk_decays[hh],
                             y_ref, r0)
    kv_ref[0:BLOCK, :] = kv_ref[ts:ts + BLOCK, :]


def _alibi_slopes():
    n = N_ATTN_HEADS
    return jnp.asarray(2.0 ** (-8.0 * (np.arange(n) + 1) / n), dtype=f32)


def _mixer_tables(sinks):
    w = BLOCK
    qi = jnp.arange(w)[:, None]
    kj = jnp.arange(2 * w)[None, :]
    dist = qi + w - kj
    valid = (dist >= 0) & (dist < w)
    valid_first = valid & (kj >= w)
    slopes = _alibi_slopes().reshape(N_KV_HEADS, PAIRS, 2)
    ali = -slopes[:, :, :, None, None] * dist.astype(f32)

    def table(mask):
        t = jnp.where(mask, ali, MASK_VALUE)
        t = jnp.transpose(t, (0, 1, 3, 2, 4))
        return t.reshape(N_KV_HEADS, PAIRS * w, 2 * 2 * w)

    bias = jnp.stack([table(valid), table(valid_first)])

    sk = sinks.astype(f32).reshape(N_KV_HEADS, PAIRS, 2)
    sk = jnp.transpose(sk, (0, 2, 1))
    sink_cols = jnp.repeat(sk, w, axis=2)[..., None]

    half = jnp.arange(LANES)[None, :] < HEAD_DIM
    top = jnp.arange(2 * 2 * w)[:, None] < 2 * w
    ones_cols = jnp.where(top == half, 1.0, 0.0).astype(bf16)

    log_gamma = jnp.log(1.0 - 2.0 ** (-5.0 - jnp.arange(N_RET_HEADS, dtype=f32)))
    idx = jnp.arange(w, dtype=f32)
    rel = idx[:, None] - idx[None, :]
    dec = jnp.where(rel >= 0, jnp.exp(log_gamma[:, None, None] * jnp.maximum(rel, 0.0)), 0.0)
    zeta = jnp.exp(log_gamma[:, None] * (w - 1.0 - idx)[None, :])
    xi = jnp.exp(log_gamma[:, None] * (idx + 1.0)[None, :])
    zeta_t = jnp.broadcast_to(zeta[:, :, None], (N_RET_HEADS, w, RET_DIM))
    xi_t = jnp.broadcast_to(xi[:, :, None], (N_RET_HEADS, w, RET_DIM))
    return bias, sink_cols, ones_cols, dec, zeta_t, xi_t


def _chunk_decays():
    gamma = 1.0 - 2.0 ** (-5.0 - np.arange(N_RET_HEADS, dtype=np.float64))
    return tuple(float(np.float32(np.exp(np.float32(np.log(np.float32(g))) * np.float32(BLOCK))))
                 for g in gamma)


def _mixer_call(proj, tables):
    s, n = proj.shape
    ts = MIXER_TS
    bias, sink_cols, ones_cols, dec, zeta_t, xi_t = tables
    full = lambda a: pl.BlockSpec(a.shape, lambda i, _nd=a.ndim: (0,) * _nd)
    table_bytes = (bias.size * 4 + sink_cols.size * LANES * 4 + ones_cols.size * 2
                   + dec.size * 4 + zeta_t.size * 4 + xi_t.size * 4)
    buffers = (2 * ts * n * 2 + 2 * ts * MIX_WIDTH * 2 + 2 * table_bytes
               + (BLOCK + ts) * 2 * LANES * 2 + N_RET_HEADS * RET_DIM * RET_DIM * 4)
    return pl.pallas_call(
        functools.partial(_mixer_kernel, chunk_decays=_chunk_decays()),
        out_shape=jax.ShapeDtypeStruct((s, MIX_WIDTH), bf16),
        grid=(s // ts,),
        in_specs=[pl.BlockSpec((ts, n), lambda i: (i, 0)), full(bias), full(sink_cols),
                  full(ones_cols), full(dec), full(zeta_t), full(xi_t)],
        out_specs=pl.BlockSpec((ts, MIX_WIDTH), lambda i: (i, 0)),
        scratch_shapes=[pltpu.VMEM((BLOCK + ts, 2 * LANES), bf16),
                        pltpu.VMEM((N_RET_HEADS, RET_DIM, RET_DIM), f32)],
        compiler_params=pltpu.CompilerParams(
            dimension_semantics=("arbitrary",),
            vmem_limit_bytes=_vmem_limit(buffers),
        ),
        name="mixer_core",
    )(proj, bias, sink_cols, ones_cols, dec, zeta_t, xi_t)


def _vec_block(norm_pre, norm_post, mods, l, sub):
    rows = [norm_pre[l, sub], mods[l, sub, 0], mods[l, sub, 1], mods[l, sub, 2], norm_post[l, sub]]
    pad = jnp.zeros((VEC_ROWS - len(rows), D_MODEL), f32)
    return jnp.concatenate([jnp.stack(rows), pad], axis=0)


def kernel(x, c, w_ada, b_ada, norm_pre, norm_post, ffn1_gate, ffn1_up, ffn1_down,
           w_in, w_out, attn_sinks, ffn2_gate, ffn2_up, ffn2_down):
    assert x.shape == (1, SEQ, D_MODEL) and c.shape == (1, D_MODEL)
    xs = x.reshape(SEQ, D_MODEL)
    mods = _mods_call(c.reshape(D_MODEL, 1), w_ada, b_ada.reshape(DEPTH, 1, -1))
    mods = mods.reshape(DEPTH, N_SUBLAYERS, N_MOD, D_MODEL)
    for l in range(DEPTH):
        vec = [_vec_block(norm_pre, norm_post, mods, l, sub) for sub in range(N_SUBLAYERS)]
        xs = _ffn_call(xs, vec[0], ffn1_gate[l].astype(bf16), ffn1_up[l].astype(bf16),
                       ffn1_down[l].astype(bf16), 0.5)
        proj = _inproj_call(xs, vec[1], w_in[l].astype(bf16))
        y = _mixer_call(proj, _mixer_tables(attn_sinks[l]))
        xs = _outproj_call(y, xs, vec[1], w_out[l].astype(bf16))
        xs = _ffn_call(xs, vec[2], ffn2_gate[l].astype(bf16), ffn2_up[l].astype(bf16),
                       ffn2_down[l].astype(bf16), 0.5)
    return xs.reshape(1, SEQ, D_MODEL)
```
